```python
import math
import jax, jax.numpy as jnp
from jax import lax
import numpy as np

D_MODEL = 2048
BATCH = 2
SEQ = 4096
DEPTH = 1
DEC_BATCH = 128
DEC_SEQ = 1
PAST_LEN = 8192
PAGE_SIZE = 128

GLA_HEADS = 4
GLA_DK = D_MODEL // 2 // GLA_HEADS
GLA_DV = D_MODEL // GLA_HEADS
GLA_GATE_RANK = 16
GLA_GATE_NORMALIZER = 16.0
GLA_CHUNK = 64
MLA_HEADS = D_MODEL // 128
MLA_Q_RANK = 512
MLA_KV_RANK = 512
MLA_NOPE = 128
MLA_ROPE = 64
MLA_V = 128
MLA_SCALE = (MLA_NOPE + MLA_ROPE) ** -0.5
ROPE_THETA = 10000.0
Q_BLOCK = 128
D_FF = ((8 * D_MODEL // 3 + 255) // 256) * 256
FFN_RES = 0.5
EPS = 1e-6
IN_SPLITS = (GLA_HEADS * GLA_DK, GLA_HEADS * GLA_DK, GLA_HEADS * GLA_DV, GLA_HEADS * GLA_DV,
             GLA_GATE_RANK, MLA_Q_RANK, MLA_KV_RANK, MLA_ROPE, D_MODEL, D_MODEL)

kernel_name = "gla_mla_parallel_macaron_decode_step"


def rmsnorm(x, g):
    xf = x.astype(jnp.float32)
    y = xf * lax.rsqrt(jnp.mean(xf * xf, axis=-1, keepdims=True) + EPS)
    return (y * g.astype(jnp.float32)).astype(x.dtype)


def ffn_sublayer(x, g_pre, g_post, w_gate, w_up, w_down):
    h = rmsnorm(x, g_pre)
    f = (jax.nn.silu(h @ w_gate) * (h @ w_up)) @ w_down
    return x + FFN_RES * rmsnorm(f, g_post)


def rope(x, pos):
    half = x.shape[-1] // 2
    inv = jnp.power(ROPE_THETA, -jnp.arange(half, dtype=jnp.float32) / half)
    ang = pos.astype(jnp.float32)[:, None] * inv[None, :]
    cos = jnp.cos(ang)[None, :, None, :]
    sin = jnp.sin(ang)[None, :, None, :]
    xf = x.astype(jnp.float32)
    x1, x2 = xf[..., :half], xf[..., half:]
    return jnp.concatenate([x1 * cos - x2 * sin, x1 * sin + x2 * cos], axis=-1).astype(x.dtype)


def gla_chunked(q, k, v, log_a, s0):
    B, L, H, DK = q.shape
    DV = v.shape[-1]
    C = min(GLA_CHUNK, L)
    n = -(-L // C)
    pad = n * C - L

    def blocks(t):
        t = jnp.pad(t.astype(jnp.float32), ((0, 0), (0, pad), (0, 0), (0, 0)))
        return t.reshape(B, n, C, H, t.shape[-1]).transpose(1, 0, 3, 2, 4)

    qc, kc, vc, ac = blocks(q), blocks(k), blocks(v), blocks(log_a)
    b = jnp.cumsum(ac, axis=3)
    b_last = b[:, :, :, -1:, :]
    q_in = qc * jnp.exp(b) * (DK ** -0.5)
    k_in = kc * jnp.exp(-b)
    k_end = kc * jnp.exp(b_last - b)
    decay = jnp.exp(b_last[:, :, :, 0, :])
    causal = jnp.tril(jnp.ones((C, C), dtype=bool))

    def step(S, xs):
        qi, ki, vi, kei, di = xs
        att = jnp.where(causal, jnp.einsum('bhtd,bhsd->bhts', qi, ki), 0.0)
        o = jnp.einsum('bhtd,bhdv->bhtv', qi, S) + jnp.einsum('bhts,bhsv->bhtv', att, vi)
        S = S * di[..., :, None] + jnp.einsum('bhsd,bhsv->bhdv', kei, vi)
        return S, o

    S, o = lax.scan(step, s0.astype(jnp.float32), (q_in, k_in, vc, k_end, decay))
    o = o.transpose(1, 0, 3, 2, 4).reshape(B, n * C, H, DV)[:, :L]
    return o, S.astype(s0.dtype)


def mla_prompt_attend(q_nope, q_pe, c_kv, k_pe, w_uk, w_uv):
    B, S, H, _ = q_nope.shape
    k_nope = jnp.einsum('bsc,chd->bshd', c_kv, w_uk)
    v = jnp.einsum('bsc,chd->bshd', c_kv, w_uv)
    kpos = jnp.arange(S)

    def block(i):
        start = i * Q_BLOCK
        qn = lax.dynamic_slice_in_dim(q_nope, start, Q_BLOCK, axis=1)
        qp = lax.dynamic_slice_in_dim(q_pe, start, Q_BLOCK, axis=1)
        s = (jnp.einsum('bthd,bshd->bhts', qn, k_nope)
             + jnp.einsum('bthr,bsr->bhts', qp, k_pe)).astype(jnp.float32) * MLA_SCALE
        qpos = start + jnp.arange(Q_BLOCK)
        s = jnp.where(kpos[None, :] <= qpos[:, None], s, -jnp.inf)
        p = jax.nn.softmax(s, axis=-1).astype(v.dtype)
        return jnp.einsum('bhts,bshd->bthd', p, v)

    o = lax.map(block, jnp.arange(S // Q_BLOCK))
    return o.transpose(1, 0, 2, 3, 4).reshape(B, S, H, MLA_V)


def mla_cached_attend(q_nope, q_pe, c_kv, k_pe, past_ckv, past_kpe, w_uk, w_uv):
    T = q_nope.shape[1]
    P = past_ckv.shape[1]
    q_lat = jnp.einsum('bthd,chd->bthc', q_nope, w_uk)
    s_past = (jnp.einsum('bthc,bpc->bhtp', q_lat, past_ckv)
              + jnp.einsum('bthr,bpr->bhtp', q_pe, past_kpe)).astype(jnp.float32) * MLA_SCALE
    s_new = (jnp.einsum('bthc,bsc->bhts', q_lat, c_kv)
             + jnp.einsum('bthr,bsr->bhts', q_pe, k_pe)).astype(jnp.float32) * MLA_SCALE
    causal = jnp.tril(jnp.ones((T, T), dtype=bool))
    s_new = jnp.where(causal, s_new, -jnp.inf)
    p = jax.nn.softmax(jnp.concatenate([s_past, s_new], axis=-1), axis=-1).astype(c_kv.dtype)
    o_lat = (jnp.einsum('bhtp,bpc->bthc', p[..., :P], past_ckv)
             + jnp.einsum('bhts,bsc->bthc', p[..., P:], c_kv))
    return jnp.einsum('bthc,chd->bthd', o_lat, w_uv)


def token_mixer(h, pos, gla_s0, past, mix):
    (w_in, gla_w_gk, gla_b_gk, gla_g_norm, mla_g_q_norm, mla_w_q_up,
     mla_g_kv_norm, mla_w_uk, mla_w_uv, w_out) = mix
    B, T, _ = h.shape
    idx = [int(i) for i in np.cumsum(IN_SPLITS)[:-1]]
    (gq, gk, gv, gg, glr, cq, ckv, kpe, ga, gb) = jnp.split(h @ w_in, idx, axis=-1)
    log_a = jax.nn.log_sigmoid((glr @ gla_w_gk + gla_b_gk).astype(jnp.float32)) / GLA_GATE_NORMALIZER
    o, gla_s = gla_chunked(gq.reshape(B, T, GLA_HEADS, GLA_DK), gk.reshape(B, T, GLA_HEADS, GLA_DK),
                           gv.reshape(B, T, GLA_HEADS, GLA_DV), log_a.reshape(B, T, GLA_HEADS, GLA_DK), gla_s0)
    o = rmsnorm(o.astype(h.dtype), gla_g_norm) * jax.nn.silu(gg.reshape(B, T, GLA_HEADS, GLA_DV))
    o_a = o.reshape(B, T, GLA_HEADS * GLA_DV)
    q = (rmsnorm(cq, mla_g_q_norm) @ mla_w_q_up).reshape(B, T, MLA_HEADS, MLA_NOPE + MLA_ROPE)
    q_nope, q_pe = q[..., :MLA_NOPE], rope(q[..., MLA_NOPE:], pos)
    c_kv = rmsnorm(ckv, mla_g_kv_norm)
    k_pe = rope(kpe[:, :, None, :], pos)[:, :, 0, :]
    if past is None:
        ob = mla_prompt_attend(q_nope, q_pe, c_kv, k_pe, mla_w_uk, mla_w_uv)
    else:
        ob = mla_cached_attend(q_nope, q_pe, c_kv, k_pe, past[0], past[1], mla_w_uk, mla_w_uv)
    o_b = ob.reshape(B, T, MLA_HEADS * MLA_V)
    m = jax.nn.sigmoid(ga) * o_a + jax.nn.sigmoid(gb) * o_b
    return m @ w_out, c_kv, k_pe, gla_s


def decoder_layer(x, pos, gla_s0, past, mix_g_pre, mix_g_post, ffn1, mix, ffn2):
    x = ffn_sublayer(x, *ffn1)
    m, c_kv, k_pe, gla_s = token_mixer(rmsnorm(x, mix_g_pre), pos, gla_s0, past, mix)
    x = x + rmsnorm(m, mix_g_post)
    x = ffn_sublayer(x, *ffn2)
    return x, c_kv, k_pe, gla_s


def setup_inputs(seed: int = 0) -> dict:
    key = jax.random.key(seed)
    ks = iter(jax.random.split(key, 40))
    f32 = jnp.float32
    n_pages = PAST_LEN // PAGE_SIZE
    n_pool = (5 * DEC_BATCH * n_pages + 3) // 4
    d_in = sum(IN_SPLITS)

    def nrm(shape, scale):
        return jax.random.normal(next(ks), shape, f32) * scale

    def gain(n):
        return 1.0 + nrm((DEPTH, n), 0.02)

    perm = jax.random.permutation(next(ks), n_pool)
    page_table = perm[: DEC_BATCH * n_pages].reshape(DEC_BATCH, n_pages).astype(jnp.int32)
    return {
        "x_prompt": nrm((BATCH, SEQ, D_MODEL), 1.0),
        "x_sample": nrm((DEC_BATCH, DEC_SEQ, D_MODEL), 1.0),
        "cache_ckv": nrm((DEPTH, n_pool, PAGE_SIZE, MLA_KV_RANK), 1.0),
        "cache_kpe": nrm((DEPTH, n_pool, PAGE_SIZE, MLA_ROPE), 1.0),
        "state_gla": nrm((DEPTH, DEC_BATCH, GLA_HEADS, GLA_DK, GLA_DV), 1.0),
        "page_table": page_table,
        "ffn1_g_pre": gain(D_MODEL),
        "ffn1_g_post": gain(D_MODEL),
        "ffn1_w_gate": nrm((DEPTH, D_MODEL, D_FF), D_MODEL ** -0.5),
        "ffn1_w_up": nrm((DEPTH, D_MODEL, D_FF), D_MODEL ** -0.5),
        "ffn1_w_down": nrm((DEPTH, D_FF, D_MODEL), D_FF ** -0.5),
        "mix_g_pre": gain(D_MODEL),
        "mix_g_post": gain(D_MODEL),
        "w_in": nrm((DEPTH, D_MODEL, d_in), D_MODEL ** -0.5),
        "gla_w_gk": nrm((DEPTH, GLA_GATE_RANK, GLA_HEADS * GLA_DK), GLA_GATE_RANK ** -0.5),
        "gla_b_gk": nrm((DEPTH, GLA_HEADS * GLA_DK), 0.1),
        "gla_g_norm": gain(GLA_DV),
        "mla_g_q_norm": gain(MLA_Q_RANK),
        "mla_w_q_up": nrm((DEPTH, MLA_Q_RANK, MLA_HEADS * (MLA_NOPE + MLA_ROPE)), MLA_Q_RANK ** -0.5),
        "mla_g_kv_norm": gain(MLA_KV_RANK),
        "mla_w_uk": nrm((DEPTH, MLA_KV_RANK, MLA_HEADS, MLA_NOPE), MLA_KV_RANK ** -0.5),
        "mla_w_uv": nrm((DEPTH, MLA_KV_RANK, MLA_HEADS, MLA_V), MLA_KV_RANK ** -0.5),
        "w_out": nrm((DEPTH, D_MODEL, D_MODEL), D_MODEL ** -0.5),
        "ffn2_g_pre": gain(D_MODEL),
        "ffn2_g_post": gain(D_MODEL),
        "ffn2_w_gate": nrm((DEPTH, D_MODEL, D_FF), D_MODEL ** -0.5),
        "ffn2_w_up": nrm((DEPTH, D_MODEL, D_FF), D_MODEL ** -0.5),
        "ffn2_w_down": nrm((DEPTH, D_FF, D_MODEL), D_FF ** -0.5),
    }


def reference(x_prompt, x_sample, cache_ckv, cache_kpe, state_gla, page_table,
              ffn1_g_pre, ffn1_g_post, ffn1_w_gate, ffn1_w_up, ffn1_w_down,
              mix_g_pre, mix_g_post, w_in, gla_w_gk, gla_b_gk, gla_g_norm,
              mla_g_q_norm, mla_w_q_up, mla_g_kv_norm, mla_w_uk, mla_w_uv, w_out,
              ffn2_g_pre, ffn2_g_post, ffn2_w_gate, ffn2_w_up, ffn2_w_down):
    B, S, _ = x_prompt.shape
    Bd, T, _ = x_sample.shape
    n_pages = page_table.shape[1]
    past_len = n_pages * PAGE_SIZE
    pos_prompt = jnp.arange(S)
    pos_sample = past_len + jnp.arange(T)
    yp, ys = x_prompt, x_sample
    ckv_p, kpe_p, gla_p, ckv_s, kpe_s, gla_s = [], [], [], [], [], []
    for l in range(DEPTH):
        ffn1 = (ffn1_g_pre[l], ffn1_g_post[l], ffn1_w_gate[l], ffn1_w_up[l], ffn1_w_down[l])
        mix = (w_in[l], gla_w_gk[l], gla_b_gk[l], gla_g_norm[l], mla_g_q_norm[l], mla_w_q_up[l],
               mla_g_kv_norm[l], mla_w_uk[l], mla_w_uv[l], w_out[l])
        ffn2 = (ffn2_g_pre[l], ffn2_g_post[l], ffn2_w_gate[l], ffn2_w_up[l], ffn2_w_down[l])
        s0 = jnp.zeros((B, GLA_HEADS, GLA_DK, GLA_DV), x_prompt.dtype)
        yp, c1, k1, g1 = decoder_layer(yp, pos_prompt, s0, None, mix_g_pre[l], mix_g_post[l], ffn1, mix, ffn2)
        past = (cache_ckv[l, page_table].reshape(Bd, past_len, MLA_KV_RANK),
                cache_kpe[l, page_table].reshape(Bd, past_len, MLA_ROPE))
        ys, c2, k2, g2 = decoder_layer(ys, pos_sample, state_gla[l], past, mix_g_pre[l], mix_g_post[l], ffn1, mix, ffn2)
        ckv_p.append(c1); kpe_p.append(k1); gla_p.append(g1)
        ckv_s.append(c2); kpe_s.append(k2); gla_s.append(g2)
    return (yp, ys, jnp.stack(ckv_p), jnp.stack(kpe_p), jnp.stack(gla_p),
            jnp.stack(ckv_s), jnp.stack(kpe_s), jnp.stack(gla_s))
```

```python
import functools

import jax
import jax.numpy as jnp
from jax import lax
from jax.experimental import pallas as pl
from jax.experimental.pallas import tpu as pltpu

F32 = jnp.float32
BF16 = jnp.bfloat16

EPS = 1e-6
FFN_RES = 0.5
GLA_GATE_NORMALIZER = 16.0
GLA_CHUNK = 64
ROPE_THETA = 10000.0

V7X_VMEM_BYTES = 64 * 1024 * 1024
VMEM_LIMIT_BYTES = V7X_VMEM_BYTES - 8 * 1024 * 1024
LANES = 128


def _cparams(*sem):
    return pltpu.CompilerParams(dimension_semantics=sem, vmem_limit_bytes=VMEM_LIMIT_BYTES)


def _dot(a, b):
    return jnp.dot(a, b, preferred_element_type=F32)


def _dot_nt(a, b):
    return lax.dot_general(a, b, (((1,), (1,)), ((), ())), preferred_element_type=F32)


def _rms(x, g):
    return x * lax.rsqrt(jnp.mean(x * x, axis=-1, keepdims=True) + EPS) * g


def _split3(x):
    hi = x.astype(BF16)
    r1 = x - hi.astype(F32)
    mid = r1.astype(BF16)
    lo = (r1 - mid.astype(F32)).astype(BF16)
    return hi, mid, lo


def _log_sigmoid(x):
    return jnp.minimum(x, 0.0) - jnp.log1p(jnp.exp(-jnp.abs(x)))


def _tile(n, pref):
    t = min(n, pref)
    while n % t:
        t //= 2
    return t


def _ffn_kernel(has_next, x_ref, gpre_ref, gpost_ref, wg_ref, wu_ref, wd_ref, *rest):
    if has_next:
        gnext_ref, out_ref, hnext_ref, h_sc, acc_sc = rest
    else:
        out_ref, h_sc, acc_sc = rest
    f = pl.program_id(1)

    @pl.when(f == 0)
    def _():
        h_sc[...] = _rms(x_ref[...], gpre_ref[...]).astype(BF16)
        acc_sc[...] = jnp.zeros_like(acc_sc)

    h = h_sc[...]
    g = _dot(h, wg_ref[...])
    u = _dot(h, wu_ref[...])
    a = (g * jax.nn.sigmoid(g) * u).astype(BF16)
    acc_sc[...] += _dot(a, wd_ref[...])

    @pl.when(f == pl.num_programs(1) - 1)
    def _():
        y = x_ref[...] + FFN_RES * _rms(acc_sc[...], gpost_ref[...])
        out_ref[...] = y
        if has_next:
            hnext_ref[...] = _rms(y, gnext_ref[...]).astype(BF16)


def _ffn(x, g_pre, g_post, wg, wu, wd, g_next=None):
    m, d = x.shape
    dff = wg.shape[1]
    tm = _tile(m, 512)
    tf = _tile(dff, 512)
    has_next = g_next is not None
    row = lambda i, f: (i, 0)
    vec = pl.BlockSpec((1, d), lambda i, f: (0, 0))
    in_specs = [pl.BlockSpec((tm, d), row), vec, vec,
                pl.BlockSpec((d, tf), lambda i, f: (0, f)),
                pl.BlockSpec((d, tf), lambda i, f: (0, f)),
                pl.BlockSpec((tf, d), lambda i, f: (f, 0))]
    args = [x, g_pre, g_post, wg, wu, wd]
    out_shape = [jax.ShapeDtypeStruct((m, d), F32)]
    out_specs = [pl.BlockSpec((tm, d), row)]
    if has_next:
        in_specs.append(vec)
        args.append(g_next)
        out_shape.append(jax.ShapeDtypeStruct((m, d), BF16))
        out_specs.append(pl.BlockSpec((tm, d), row))
    res = pl.pallas_call(
        functools.partial(_ffn_kernel, has_next),
        grid=(m // tm, dff // tf),
        in_specs=in_specs, out_specs=out_specs, out_shape=out_shape,
        scratch_shapes=[pltpu.VMEM((tm, d), BF16), pltpu.VMEM((tm, d), F32)],
        compiler_params=_cparams("parallel", "arbitrary"),
    )(*args)
    return res if has_next else res[0]


def _matmul_kernel(x_ref, w_ref, o_ref):
    o_ref[...] = _dot(x_ref[...], w_ref[...]).astype(o_ref.dtype)


def _matmul(x, w, out_dtype=BF16, tm_pref=1024, tn_pref=1024):
    m, k = x.shape
    n = w.shape[1]
    tm = _tile(m, tm_pref)
    tn = _tile(n, tn_pref)
    return pl.pallas_call(
        _matmul_kernel,
        grid=(m // tm, n // tn),
        in_specs=[pl.BlockSpec((tm, k), lambda i, j: (i, 0)),
                  pl.BlockSpec((k, tn), lambda i, j: (0, j))],
        out_specs=pl.BlockSpec((tm, tn), lambda i, j: (i, j)),
        out_shape=jax.ShapeDtypeStruct((m, n), out_dtype),
        compiler_params=_cparams("parallel", "arbitrary"),
    )(x, w)


def _head_matmul_kernel(x_ref, w_ref, o_ref):
    o_ref[...] = _dot(x_ref[...], w_ref[0]).astype(o_ref.dtype)


def _head_matmul(x, w, out_dtype):
    m = x.shape[0]
    nh, k, n = w.shape
    return pl.pallas_call(
        _head_matmul_kernel,
        grid=(nh,),
        in_specs=[pl.BlockSpec((m, k), lambda h: (0, h)),
                  pl.BlockSpec((1, k, n), lambda h: (h, 0, 0))],
        out_specs=pl.BlockSpec((m, n), lambda h: (0, h)),
        out_shape=jax.ShapeDtypeStruct((m, nh * n), out_dtype),
        compiler_params=_cparams("arbitrary"),
    )(x, w)


def _latent_kernel(qr, kvr, rope, x_ref, w_ref, gq_ref, gkv_ref, cos_ref, sin_ref,
                   cqn_ref, ckv_ref, ckvb_ref, kpe_ref, tail_ref):
    z = _dot(x_ref[...], w_ref[...])
    cqn_ref[...] = _rms(z[:, :qr], gq_ref[...]).astype(BF16)
    ckv = _rms(z[:, qr:qr + kvr], gkv_ref[...])
    ckv_ref[...] = ckv
    ckvb_ref[...] = ckv.astype(BF16)
    tail = z[:, qr + kvr:]
    tail_ref[...] = tail
    kpe = tail[:, :rope]
    swapped = jnp.concatenate([kpe[:, rope // 2:], kpe[:, :rope // 2]], axis=-1)
    kpe_ref[...] = kpe * cos_ref[...] + swapped * sin_ref[...]


def _latent_proj(h, w_lat, g_q, g_kv, cos_k, sin_k, n_pos, qr, kvr, rope):
    m, d = h.shape
    n = w_lat.shape[1]
    tm = _tile(min(m, n_pos), 512)
    npb = n_pos // tm
    row = lambda i: (i, 0)
    pos = lambda i: (i % npb, 0)
    full = lambda i: (0, 0)
    return pl.pallas_call(
        functools.partial(_latent_kernel, qr, kvr, rope),
        grid=(m // tm,),
        in_specs=[pl.BlockSpec((tm, d), row), pl.BlockSpec((d, n), full),
                  pl.BlockSpec((1, qr), full), pl.BlockSpec((1, kvr), full),
                  pl.BlockSpec((tm, rope), pos), pl.BlockSpec((tm, rope), pos)],
        out_specs=[pl.BlockSpec((tm, qr), row), pl.BlockSpec((tm, kvr), row),
                   pl.BlockSpec((tm, kvr), row), pl.BlockSpec((tm, rope), row),
                   pl.BlockSpec((tm, LANES), row)],
        out_shape=[jax.ShapeDtypeStruct((m, qr), BF16), jax.ShapeDtypeStruct((m, kvr), F32),
                   jax.ShapeDtypeStruct((m, kvr), BF16), jax.ShapeDtypeStruct((m, rope), F32),
                   jax.ShapeDtypeStruct((m, LANES), F32)],
        compiler_params=_cparams("parallel"),
    )(h, w_lat, g_q, g_kv, cos_k, sin_k)


def _qup_kernel(n_nope, x_ref, w_ref, cos_ref, sin_ref, qn_ref, qp_ref):
    z = _dot(x_ref[...], w_ref[...])
    qn_ref[...] = z[:, :n_nope].astype(BF16)
    half = (z.shape[1] - n_nope) // 2
    p1 = z[:, n_nope:n_nope + half]
    p2 = z[:, n_nope + half:]
    c = cos_ref[...]
    s = sin_ref[...]
    qp_ref[:, :half] = (p1 * c - p2 * s).astype(BF16)
    qp_ref[:, half:] = (p1 * s + p2 * c).astype(BF16)


def _q_up(cqn, w_q, cos_q, sin_q, n_pos, n_nope):
    m, k = cqn.shape
    n = w_q.shape[1]
    n_pe = n - n_nope
    tm = _tile(min(m, n_pos), 512)
    npb = n_pos // tm
    row = lambda i: (i, 0)
    pos = lambda i: (i % npb, 0)
    return pl.pallas_call(
        functools.partial(_qup_kernel, n_nope),
        grid=(m // tm,),
        in_specs=[pl.BlockSpec((tm, k), row), pl.BlockSpec((k, n), lambda i: (0, 0)),
                  pl.BlockSpec((tm, n_pe // 2), pos), pl.BlockSpec((tm, n_pe // 2), pos)],
        out_specs=[pl.BlockSpec((tm, n_nope), row), pl.BlockSpec((tm, n_pe), row)],
        out_shape=[jax.ShapeDtypeStruct((m, n_nope), BF16), jax.ShapeDtypeStruct((m, n_pe), BF16)],
        compiler_params=_cparams("parallel"),
    )(cqn, w_q, cos_q, sin_q)


def _gla_chunk_kernel(n_chunks, dk, q_ref, k_ref, v_ref, gg_ref, tail_ref, wgk_ref, bgk_ref,
                      gnorm_ref, o_ref, s_ref):
    c = GLA_CHUNK

    @pl.when(pl.program_id(2) == 0)
    def _():
        s_ref[...] = jnp.zeros_like(s_ref)

    r_i = lax.broadcasted_iota(jnp.int32, (c, c), 0)
    c_i = lax.broadcasted_iota(jnp.int32, (c, c), 1)
    causal = c_i <= r_i
    tril = jnp.where(causal, 1.0, 0.0).astype(BF16)
    e_r = lax.broadcasted_iota(jnp.int32, (dk, dk), 0)
    e_c = lax.broadcasted_iota(jnp.int32, (dk, dk), 1)
    eye = jnp.where(e_r == e_c, 1.0, 0.0).astype(BF16)
    wgk = wgk_ref[...]
    bgk = bgk_ref[...]
    gnorm = gnorm_ref[...]
    scale = dk ** -0.5

    for ci in range(n_chunks):
        rows = pl.ds(ci * c, c)
        la = _log_sigmoid(_dot(tail_ref[0, rows, :].astype(BF16), wgk) + bgk) / GLA_GATE_NORMALIZER
        hi, mid, lo = _split3(la)
        cs = _dot(tril, jnp.concatenate([hi, mid, lo], axis=-1))
        b = cs[:, :dk] + cs[:, dk:2 * dk] + cs[:, 2 * dk:]
        b_last = b[c - 1:c, :]
        q = q_ref[0, rows, :].astype(F32)
        k = k_ref[0, rows, :].astype(F32)
        v = v_ref[0, rows, :]
        q_in = (q * jnp.exp(b) * scale).astype(BF16)
        k_in = (k * jnp.exp(-b)).astype(BF16)
        k_end = (k * jnp.exp(b_last - b)).astype(BF16)
        bh, bm, bl = _split3(jnp.broadcast_to(b_last, (16, dk)))
        bt = _dot_nt(eye, jnp.concatenate([bh, bm, bl], axis=0))
        decay_col = jnp.exp(bt[:, 0:1] + bt[:, 16:17] + bt[:, 32:33])
        k_end_t = _dot_nt(eye, k_end).astype(BF16)
        att = jnp.where(causal, _dot_nt(q_in, k_in), 0.0).astype(BF16)
        s_old = s_ref[0, 0]
        o = _dot(q_in, s_old.astype(BF16)) + _dot(att, v)
        s_ref[0, 0] = s_old * decay_col + _dot(k_end_t, v)
        gg = gg_ref[0, rows, :].astype(F32)
        o_ref[0, rows, :] = (_rms(o, gnorm) * (gg * jax.nn.sigmoid(gg))).astype(BF16)


def _gla_prompt(z_gla, tail, wgk_pad, b_gk, g_norm, nh, dk, dv):
    bsz, s, _ = z_gla.shape
    tc = _tile(s, 4 * GLA_CHUNK)
    nkb = nh * dk // dk
    voff = 2 * nh * dk // dv
    return pl.pallas_call(
        functools.partial(_gla_chunk_kernel, tc // GLA_CHUNK, dk),
        grid=(bsz, nh, s // tc),
        in_specs=[pl.BlockSpec((1, tc, dk), lambda b, h, i: (b, i, h)),
                  pl.BlockSpec((1, tc, dk), lambda b, h, i: (b, i, nkb + h)),
                  pl.BlockSpec((1, tc, dv), lambda b, h, i: (b, i, voff + h)),
                  pl.BlockSpec((1, tc, dv), lambda b, h, i: (b, i, voff + nh + h)),
                  pl.BlockSpec((1, tc, LANES), lambda b, h, i: (b, i, 0)),
                  pl.BlockSpec((LANES, dk), lambda b, h, i: (0, h)),
                  pl.BlockSpec((1, dk), lambda b, h, i: (0, h)),
                  pl.BlockSpec((1, dv), lambda b, h, i: (0, 0))],
        out_specs=[pl.BlockSpec((1, tc, dv), lambda b, h, i: (b, i, h)),
                   pl.BlockSpec((1, 1, dk, dv), lambda b, h, i: (b, h, 0, 0))],
        out_shape=[jax.ShapeDtypeStruct((bsz, s, nh * dv), BF16),
                   jax.ShapeDtypeStruct((bsz, nh, dk, dv), F32)],
        compiler_params=_cparams("parallel", "parallel", "arbitrary"),
    )(z_gla, z_gla, z_gla, z_gla, tail, wgk_pad, b_gk, g_norm)


def _gla_gate_kernel(tail_ref, wgk_ref, bgk_ref, hi_ref, mid_ref, lo_ref):
    la = _log_sigmoid(_dot(tail_ref[...].astype(BF16), wgk_ref[...]) + bgk_ref[...]) / GLA_GATE_NORMALIZER
    hi, mid, lo = _split3(jnp.exp(la))
    hi_ref[...] = hi
    mid_ref[...] = mid
    lo_ref[...] = lo


def _gla_gate(tail, wgk_pad, b_gk):
    m = tail.shape[0]
    n = wgk_pad.shape[1]
    full = lambda i: (0, 0)
    return pl.pallas_call(
        _gla_gate_kernel,
        grid=(1,),
        in_specs=[pl.BlockSpec((m, LANES), full), pl.BlockSpec((LANES, n), full), pl.BlockSpec((1, n), full)],
        out_specs=[pl.BlockSpec((m, n), full)] * 3,
        out_shape=[jax.ShapeDtypeStruct((m, n), BF16)] * 3,
        compiler_params=_cparams("arbitrary"),
    )(tail, wgk_pad, b_gk)


def _gla_step_kernel(nh, dk, stack_ref, v_ref, gg_ref, gnorm_ref, s0_ref, o_ref, s_ref):
    e_r = lax.broadcasted_iota(jnp.int32, (dk, dk), 0)
    e_c = lax.broadcasted_iota(jnp.int32, (dk, dk), 1)
    eye = jnp.where(e_r == e_c, 1.0, 0.0).astype(BF16)
    gnorm = gnorm_ref[...]
    scale = dk ** -0.5
    for h in range(nh):
        cols = _dot_nt(eye, stack_ref[0, h])
        q_col = cols[:, 0:1]
        k_col = cols[:, 1:2]
        a_col = cols[:, 2:3] + cols[:, 3:4] + cols[:, 4:5]
        s_new = s0_ref[0, h] * a_col + k_col * v_ref[0, h]
        s_ref[0, h] = s_new
        o = jnp.sum(s_new * q_col, axis=0, keepdims=True) * scale
        gg = gg_ref[0, h]
        o_ref[0, h] = _rms(o, gnorm) * (gg * jax.nn.sigmoid(gg))


def _gla_step(stack, v, gg, g_norm, s0):
    bd, nh, dk, dv = s0.shape
    return pl.pallas_call(
        functools.partial(_gla_step_kernel, nh, dk),
        grid=(bd,),
        in_specs=[pl.BlockSpec((1, nh, 16, dk), lambda b: (b, 0, 0, 0)),
                  pl.BlockSpec((1, nh, 1, dv), lambda b: (b, 0, 0, 0)),
                  pl.BlockSpec((1, nh, 1, dv), lambda b: (b, 0, 0, 0)),
                  pl.BlockSpec((1, dv), lambda b: (0, 0)),
                  pl.BlockSpec((1, nh, dk, dv), lambda b: (b, 0, 0, 0))],
        out_specs=[pl.BlockSpec((1, nh, 1, dv), lambda b: (b, 0, 0, 0)),
                   pl.BlockSpec((1, nh, dk, dv), lambda b: (b, 0, 0, 0))],
        out_shape=[jax.ShapeDtypeStruct((bd, nh, 1, dv), F32),
                   jax.ShapeDtypeStruct((bd, nh, dk, dv), F32)],
        compiler_params=_cparams("parallel"),
    )(stack, v, gg, g_norm, s0)


def _flash_kernel(tq, nope, scale, qn_ref, qp_ref, kn_ref, kp_ref, v_ref, o_ref, k_sc):
    qi = pl.program_id(2)

    @pl.when(qi == 0)
    def _():
        k_sc[:, :nope] = kn_ref[0]
        k_sc[:, nope:] = kp_ref[0]

    q = jnp.concatenate([qn_ref[0], qp_ref[0, 0]], axis=-1)
    dv = v_ref.shape[-1]

    def block(j, carry, masked):
        m, l, acc = carry
        rows = pl.ds(pl.multiple_of(j * tq, tq), tq)
        s = _dot_nt(q, k_sc[rows, :]) * scale
        if masked:
            r_i = lax.broadcasted_iota(jnp.int32, (tq, tq), 0)
            c_i = lax.broadcasted_iota(jnp.int32, (tq, tq), 1)
            s = jnp.where(c_i <= r_i, s, -jnp.inf)
        m_new = jnp.maximum(m, jnp.max(s, axis=-1, keepdims=True))
        alpha = jnp.exp(m - m_new)
        p = jnp.exp(s - m_new)
        l = alpha * l + jnp.sum(p, axis=-1, keepdims=True)
        acc = alpha * acc + _dot(p.astype(BF16), v_ref[0, rows, :])
        return m_new, l, acc

    init = (jnp.full((tq, 1), -jnp.inf, F32), jnp.zeros((tq, 1), F32), jnp.zeros((tq, dv), F32))
    carry = lax.fori_loop(0, qi, lambda j, cr: block(j, cr, False), init)
    _, l, acc = block(qi, carry, True)
    o_ref[0] = (acc / l).astype(o_ref.dtype)


def _flash(qn, qp, kv, kpe, nh, nope, dv, scale):
    bsz, s, _ = qn.shape
    rope = kpe.shape[-1]
    tq = _tile(s, 512)
    return pl.pallas_call(
        functools.partial(_flash_kernel, tq, nope, scale),
        grid=(bsz, nh, s // tq),
        in_specs=[pl.BlockSpec((1, tq, nope), lambda b, h, i: (b, i, h)),
                  pl.BlockSpec((1, 1, tq, rope), lambda b, h, i: (b, h, i, 0)),
                  pl.BlockSpec((1, s, nope), lambda b, h, i: (b, 0, h)),
                  pl.BlockSpec((1, s, rope), lambda b, h, i: (b, 0, 0)),
                  pl.BlockSpec((1, s, dv), lambda b, h, i: (b, 0, nh * nope // dv + h))],
        out_specs=pl.BlockSpec((1, tq, dv), lambda b, h, i: (b, i, h)),
        out_shape=jax.ShapeDtypeStruct((bsz, s, nh * dv), BF16),
        scratch_shapes=[pltpu.VMEM((s, nope + rope), BF16)],
        compiler_params=_cparams("parallel", "parallel", "arbitrary"),
    )(qn, qp, kv, kpe, kv)


def _paged_kernel(pps, page, scale, pt_ref, ql_ref, qp_ref, cnew_ref, knew_ref, *rest):
    ckv_refs = rest[:pps]
    kpe_refs = rest[pps:2 * pps]
    o_ref, c_sc, k_sc, m_sc, l_sc, acc_sc = rest[2 * pps:]
    j = pl.program_id(1)

    @pl.when(j == 0)
    def _():
        m_sc[...] = jnp.full_like(m_sc, -jnp.inf)
        l_sc[...] = jnp.zeros_like(l_sc)
        acc_sc[...] = jnp.zeros_like(acc_sc)

    for p in range(pps):
        c_sc[p * page:(p + 1) * page, :] = ckv_refs[p][0, 0].astype(BF16)
        k_sc[p * page:(p + 1) * page, :] = kpe_refs[p][0, 0].astype(BF16)

    ql = ql_ref[0]
    qp = qp_ref[0]
    c = c_sc[...]
    s = (_dot_nt(ql, c) + _dot_nt(qp, k_sc[...])) * scale
    m_old = m_sc[...]
    m_new = jnp.maximum(m_old, jnp.max(s, axis=-1, keepdims=True))
    alpha = jnp.exp(m_old - m_new)
    p_ = jnp.exp(s - m_new)
    l_sc[...] = alpha * l_sc[...] + jnp.sum(p_, axis=-1, keepdims=True)
    acc_sc[...] = alpha * acc_sc[...] + _dot(p_.astype(BF16), c)
    m_sc[...] = m_new

    @pl.when(j == pl.num_programs(1) - 1)
    def _():
        cn = cnew_ref[0].astype(BF16).astype(F32)
        kn = knew_ref[0].astype(BF16).astype(F32)
        s_n = (jnp.sum(ql.astype(F32) * cn, axis=-1, keepdims=True)
               + jnp.sum(qp.astype(F32) * kn, axis=-1, keepdims=True)) * scale
        m_o = m_sc[...]
        m_n = jnp.maximum(m_o, s_n)
        a = jnp.exp(m_o - m_n)
        p_n = jnp.exp(s_n - m_n)
        l = a * l_sc[...] + p_n
        acc = a * acc_sc[...] + p_n.astype(BF16).astype(F32) * cn
        o_ref[0] = (acc / l).astype(o_ref.dtype)


def _paged_attention(q_lat, q_pe, c_new, k_new, cache_ckv, cache_kpe, layer, page_table, scale):
    bd, nh, kvr = q_lat.shape
    rope = q_pe.shape[-1]
    n_pages = page_table.shape[1]
    page = cache_ckv.shape[2]
    pps = _tile(n_pages, 16)

    def page_spec(width, p):
        return pl.BlockSpec((1, 1, page, width),
                            lambda b, j, pt: (layer, pt[b * n_pages + j * pps + p], 0, 0))

    in_specs = [pl.BlockSpec((1, nh, kvr), lambda b, j, pt: (b, 0, 0)),
                pl.BlockSpec((1, nh, rope), lambda b, j, pt: (b, 0, 0)),
                pl.BlockSpec((1, 1, kvr), lambda b, j, pt: (b, 0, 0)),
                pl.BlockSpec((1, 1, rope), lambda b, j, pt: (b, 0, 0))]
    in_specs += [page_spec(kvr, p) for p in range(pps)]
    in_specs += [page_spec(rope, p) for p in range(pps)]
    grid_spec = pltpu.PrefetchScalarGridSpec(
        num_scalar_prefetch=1,
        grid=(bd, n_pages // pps),
        in_specs=in_specs,
        out_specs=pl.BlockSpec((1, nh, kvr), lambda b, j, pt: (b, 0, 0)),
        scratch_shapes=[pltpu.VMEM((pps * page, kvr), BF16), pltpu.VMEM((pps * page, rope), BF16),
                        pltpu.VMEM((nh, 1), F32), pltpu.VMEM((nh, 1), F32), pltpu.VMEM((nh, kvr), F32)],
    )
    return pl.pallas_call(
        functools.partial(_paged_kernel, pps, page, scale),
        grid_spec=grid_spec,
        out_shape=jax.ShapeDtypeStruct((bd, nh, kvr), BF16),
        compiler_params=_cparams("parallel", "arbitrary"),
    )(page_table.reshape(-1), q_lat, q_pe, c_new, k_new,
      *([cache_ckv] * pps), *([cache_kpe] * pps))


def _merge_kernel(ga_ref, gb_ref, oa_ref, ob_ref, x_ref, w_ref, gpost_ref, out_ref):
    ga = ga_ref[...].astype(F32)
    gb = gb_ref[...].astype(F32)
    m = jax.nn.sigmoid(ga) * oa_ref[...].astype(F32) + jax.nn.sigmoid(gb) * ob_ref[...].astype(F32)
    out_ref[...] = x_ref[...] + _rms(_dot(m.astype(BF16), w_ref[...]), gpost_ref[...])


def _merge(z_gate, o_a, o_b, x, w_out, g_post):
    m, d = x.shape
    tm = _tile(m, 256)
    row = lambda i: (i, 0)
    return pl.pallas_call(
        _merge_kernel,
        grid=(m // tm,),
        in_specs=[pl.BlockSpec((tm, d), row), pl.BlockSpec((tm, d), lambda i: (i, 1)),
                  pl.BlockSpec((tm, d), row), pl.BlockSpec((tm, d), row), pl.BlockSpec((tm, d), row),
                  pl.BlockSpec((d, d), lambda i: (0, 0)), pl.BlockSpec((1, d), lambda i: (0, 0))],
        out_specs=pl.BlockSpec((tm, d), row),
        out_shape=jax.ShapeDtypeStruct((m, d), F32),
        compiler_params=_cparams("parallel"),
    )(z_gate, z_gate, o_a, o_b, x, w_out, g_post)


def _rope_tables(pos, half):
    inv = jnp.power(ROPE_THETA, -jnp.arange(half, dtype=F32) / half)
    ang = pos.astype(F32)[:, None] * inv[None, :]
    return jnp.cos(ang), jnp.sin(ang)


def _layer_weights(ffn1, mixw, ffn2, nh_mla, nope, rope):
    (w_in, gla_w_gk, gla_b_gk, gla_g_norm, g_q, w_q_up, g_kv, w_uk, w_uv, w_out, splits) = mixw
    o = [0]
    for s_ in splits:
        o.append(o[-1] + s_)
    cols = lambda a, b: w_in[:, o[a]:o[b]]
    n_tail = splits[7] + splits[4]
    w_lat = jnp.concatenate([cols(5, 8), cols(4, 5),
                             jnp.zeros((w_in.shape[0], LANES - n_tail), w_in.dtype)], axis=1).astype(BF16)
    wgk_pad = jnp.zeros((LANES, gla_w_gk.shape[1]), F32).at[splits[7]:n_tail].set(gla_w_gk).astype(BF16)
    wq = w_q_up.reshape(w_q_up.shape[0], nh_mla, nope + rope)
    half = rope // 2
    w_q = jnp.concatenate([wq[:, :, :nope].reshape(-1, nh_mla * nope),
                           wq[:, :, nope:nope + half].reshape(-1, nh_mla * half),
                           wq[:, :, nope + half:].reshape(-1, nh_mla * half)], axis=1).astype(BF16)
    kvr = w_uk.shape[0]
    return dict(
        ffn1=(ffn1[0][None], ffn1[1][None]) + tuple(w.astype(BF16) for w in ffn1[2:]),
        ffn2=(ffn2[0][None], ffn2[1][None]) + tuple(w.astype(BF16) for w in ffn2[2:]),
        w_gla=cols(0, 4).astype(BF16), w_gate=cols(8, 10).astype(BF16), w_lat=w_lat,
        wgk_pad=wgk_pad, b_gk=gla_b_gk[None], g_norm=gla_g_norm[None], g_q=g_q[None], g_kv=g_kv[None],
        w_q=w_q, w_kv=jnp.concatenate([w_uk.reshape(kvr, -1), w_uv.reshape(kvr, -1)], axis=1).astype(BF16),
        w_uk_t=jnp.transpose(w_uk, (1, 2, 0)).astype(BF16), w_uv_h=jnp.transpose(w_uv, (1, 0, 2)).astype(BF16),
        w_out=w_out.astype(BF16))


def _token_front(x, w, g_mix_pre, pos_tables, n_pos, dims):
    qr, kvr, rope, n_nope = dims
    cos_k, sin_k, cos_q, sin_q = pos_tables
    x1, h = _ffn(x, *w["ffn1"], g_next=g_mix_pre)
    z_gla = _matmul(h, w["w_gla"])
    z_gate = _matmul(h, w["w_gate"])
    cqn, c_kv, c_kv_b, k_pe, tail = _latent_proj(h, w["w_lat"], w["g_q"], w["g_kv"], cos_k, sin_k,
                                                 n_pos, qr, kvr, rope)
    qn, qp = _q_up(cqn, w["w_q"], cos_q, sin_q, n_pos, n_nope)
    return x1, z_gla, z_gate, c_kv, c_kv_b, k_pe, tail, qn, qp


def _pos_tables(pos, rope, nh_mla):
    cos, sin = _rope_tables(pos, rope // 2)
    cos_k = jnp.concatenate([cos, cos], axis=1)
    sin_k = jnp.concatenate([-sin, sin], axis=1)
    return cos_k, sin_k, jnp.tile(cos, (1, nh_mla)), jnp.tile(sin, (1, nh_mla))


def kernel(x_prompt, x_sample, cache_ckv, cache_kpe, state_gla, page_table, ffn1_g_pre, ffn1_g_post, ffn1_w_gate, ffn1_w_up, ffn1_w_down, mix_g_pre, mix_g_post, w_in, gla_w_gk, gla_b_gk, gla_g_norm, mla_g_q_norm, mla_w_q_up, mla_g_kv_norm, mla_w_uk, mla_w_uv, w_out, ffn2_g_pre, ffn2_g_post, ffn2_w_gate, ffn2_w_up, ffn2_w_down):
    bsz, seq, d = x_prompt.shape
    bd, t_new, _ = x_sample.shape
    assert t_new == 1, "the sample group is a single-token step"
    depth = w_in.shape[0]
    _, _, nh_gla, dk, dv = state_gla.shape
    rank = gla_w_gk.shape[1]
    qr = mla_g_q_norm.shape[1]
    kvr, nh_mla, nope = mla_w_uk.shape[1:]
    v_dim = mla_w_uv.shape[3]
    rope = cache_kpe.shape[-1]
    page = cache_ckv.shape[2]
    past_len = page_table.shape[1] * page
    splits = (nh_gla * dk, nh_gla * dk, nh_gla * dv, nh_gla * dv, rank, qr, kvr, rope, d, d)
    assert sum(splits) == w_in.shape[2] and rope + rank <= LANES and nope == v_dim
    mla_scale = (nope + rope) ** -0.5
    dims = (qr, kvr, rope, nh_mla * nope)

    tab_p = _pos_tables(jnp.arange(seq), rope, nh_mla)
    tab_s = _pos_tables(jnp.full((bd,), past_len, jnp.int32), rope, nh_mla)

    xp = x_prompt.reshape(bsz * seq, d)
    xs = x_sample.reshape(bd, d)
    outs = [[] for _ in range(6)]
    for l in range(depth):
        w = _layer_weights(
            (ffn1_g_pre[l], ffn1_g_post[l], ffn1_w_gate[l], ffn1_w_up[l], ffn1_w_down[l]),
            (w_in[l], gla_w_gk[l], gla_b_gk[l], gla_g_norm[l], mla_g_q_norm[l], mla_w_q_up[l],
             mla_g_kv_norm[l], mla_w_uk[l], mla_w_uv[l], w_out[l], splits),
            (ffn2_g_pre[l], ffn2_g_post[l], ffn2_w_gate[l], ffn2_w_up[l], ffn2_w_down[l]),
            nh_mla, nope, rope)
        g_mix_pre = mix_g_pre[l][None]
        g_mix_post = mix_g_post[l][None]

        x1, z_gla, z_gate, c_kv, c_kv_b, k_pe, tail, qn, qp = _token_front(xp, w, g_mix_pre, tab_p, seq, dims)
        o_a, gla_state = _gla_prompt(z_gla.reshape(bsz, seq, -1), tail.reshape(bsz, seq, LANES),
                                     w["wgk_pad"], w["b_gk"], w["g_norm"], nh_gla, dk, dv)
        kv = _matmul(c_kv_b, w["w_kv"])
        qp_h = qp.reshape(bsz, seq, 2, nh_mla, rope // 2).transpose(0, 3, 1, 2, 4).reshape(bsz, nh_mla, seq, rope)
        o_b = _flash(qn.reshape(bsz, seq, -1), qp_h, kv.reshape(bsz, seq, -1),
                     k_pe.astype(BF16).reshape(bsz, seq, rope), nh_mla, nope, v_dim, mla_scale)
        x2 = _merge(z_gate, o_a.reshape(bsz * seq, -1), o_b.reshape(bsz * seq, -1), x1, w["w_out"], g_mix_post)
        xp = _ffn(x2, *w["ffn2"])
        outs[0].append(c_kv.reshape(bsz, seq, kvr))
        outs[1].append(k_pe.reshape(bsz, seq, rope))
        outs[2].append(gla_state)

        x1, z_gla, z_gate, c_kv, c_kv_b, k_pe, tail, qn, qp = _token_front(xs, w, g_mix_pre, tab_s, bd, dims)
        a_hi, a_mid, a_lo = _gla_gate(tail, w["wgk_pad"], w["b_gk"])
        nk = nh_gla * dk
        zeros = jnp.zeros((bd, nk), BF16)
        stack = jnp.stack([z_gla[:, :nk], z_gla[:, nk:2 * nk], a_hi, a_mid, a_lo] + [zeros] * 11, axis=1)
        stack = stack.reshape(bd, 16, nh_gla, dk).transpose(0, 2, 1, 3)
        nv = nh_gla * dv
        o_a, gla_state = _gla_step(stack, z_gla[:, 2 * nk:2 * nk + nv].astype(F32).reshape(bd, nh_gla, 1, dv),
                                   z_gla[:, 2 * nk + nv:].astype(F32).reshape(bd, nh_gla, 1, dv),
                                   w["g_norm"], state_gla[l])
        q_lat = _head_matmul(qn, w["w_uk_t"], BF16)
        qp_h = qp.reshape(bd, 2, nh_mla, rope // 2).transpose(0, 2, 1, 3).reshape(bd, nh_mla, rope)
        o_lat = _paged_attention(q_lat.reshape(bd, nh_mla, kvr), qp_h, c_kv.reshape(bd, 1, kvr),
                                 k_pe.reshape(bd, 1, rope), cache_ckv, cache_kpe, l, page_table, mla_scale)
        o_b = _head_matmul(o_lat.reshape(bd, nh_mla * kvr), w["w_uv_h"], BF16)
        x2 = _merge(z_gate, o_a.reshape(bd, -1), o_b, x1, w["w_out"], g_mix_post)
        xs = _ffn(x2, *w["ffn2"])
        outs[3].append(c_kv.reshape(bd, 1, kvr))
        outs[4].append(k_pe.reshape(bd, 1, rope))
        outs[5].append(gla_state)

    return (xp.reshape(bsz, seq, d), xs.reshape(bd, 1, d)) + tuple(jnp.stack(o) for o in outs)
```
